```python
import jax, jax.numpy as jnp
from jax import lax
import numpy as np

D_MODEL = 1024
BATCH = 4
SEQ = 8192
DEPTH = 2

CHUNK = 64
QBLOCK = 128
N_A = DEPTH // 2
N_B = DEPTH - N_A
FOX_HEADS = 16
FOX_HEAD_DIM = 64
FOX_WIDTH = FOX_HEADS * FOX_HEAD_DIM
MLA_HEADS = 16
MLA_NOPE = 128
MLA_ROPE = 64
MLA_V = 128
KV_LORA = 256
Q_LORA = 768
ROPE_THETA = 10000.0
D_FF = 2816
CONV_W = 3
EPS = 1e-6

kernel_name = 'hybrid_fox_mla_yoco_convffn'


def rmsnorm(x, g):
    xf = x.astype(jnp.float32)
    y = xf * lax.rsqrt(jnp.mean(xf * xf, axis=-1, keepdims=True) + EPS)
    return (y * g.astype(jnp.float32)).astype(x.dtype)


def rope_tables(seq, dtype):
    pos = jnp.arange(seq, dtype=jnp.float32)
    inv = ROPE_THETA ** (-jnp.arange(0, MLA_ROPE, 2, dtype=jnp.float32) / MLA_ROPE)
    ang = pos[:, None] * inv[None, :]
    return jnp.cos(ang).astype(dtype), jnp.sin(ang).astype(dtype)


def apply_rope(x, cos, sin):
    x1, x2 = jnp.split(x, 2, axis=-1)
    return jnp.concatenate([x1 * cos - x2 * sin, x1 * sin + x2 * cos], axis=-1)


def to_blocks(a):
    b, s = a.shape[:2]
    a = a.reshape((b, s // QBLOCK, QBLOCK) + a.shape[2:])
    return jnp.moveaxis(a, 1, 0)


def from_blocks(a):
    a = jnp.moveaxis(a, 0, 1)
    return a.reshape((a.shape[0], a.shape[1] * a.shape[2]) + a.shape[3:])


def fox_mixer(h, w_in, b_f, w_out):
    b, s_len, _ = h.shape
    proj = h @ w_in
    qkv = proj[..., :3 * FOX_WIDTH].reshape(b, s_len, 3, FOX_HEADS, FOX_HEAD_DIM)
    q, k, v = qkv[:, :, 0], qkv[:, :, 1], qkv[:, :, 2]
    logf = jax.nn.log_sigmoid((proj[..., 3 * FOX_WIDTH:] + b_f).astype(jnp.float32))
    cum = jnp.cumsum(logf, axis=1)
    cum_k = jnp.transpose(cum, (0, 2, 1))
    kpos = jnp.arange(s_len)
    scale = FOX_HEAD_DIM ** -0.5

    def block(args):
        i, qi, ci = args
        sc = jnp.einsum('bqhd,bkhd->bhqk', qi, k, preferred_element_type=jnp.float32) * scale
        sc = sc + (jnp.transpose(ci, (0, 2, 1))[..., :, None] - cum_k[:, :, None, :])
        qpos = i * QBLOCK + jnp.arange(QBLOCK)
        sc = jnp.where(kpos[None, :] <= qpos[:, None], sc, -jnp.inf)
        p = jax.nn.softmax(sc, axis=-1).astype(v.dtype)
        return jnp.einsum('bhqk,bkhd->bqhd', p, v)

    o = lax.map(block, (jnp.arange(s_len // QBLOCK), to_blocks(q), to_blocks(cum)))
    return from_blocks(o).reshape(b, s_len, FOX_WIDTH) @ w_out


def mla_shared_kv(x, kv_in_g, w_dkv, kv_norm_g, w_uk, w_uv, cos, sin):
    b, s_len, _ = x.shape
    ckv = rmsnorm(x, kv_in_g) @ w_dkv
    latent = rmsnorm(ckv[..., :KV_LORA], kv_norm_g)
    k_rope = apply_rope(ckv[..., KV_LORA:], cos, sin)
    k_nope = (latent @ w_uk).reshape(b, s_len, MLA_HEADS, MLA_NOPE)
    v = (latent @ w_uv).reshape(b, s_len, MLA_HEADS, MLA_V)
    return k_nope, k_rope, v


def mla_mixer(h, w_dq, q_norm_g, w_uq, w_out, k_nope, k_rope, v, cos, sin):
    b, s_len, _ = h.shape
    q = (rmsnorm(h @ w_dq, q_norm_g) @ w_uq).reshape(b, s_len, MLA_HEADS, MLA_NOPE + MLA_ROPE)
    q_nope = q[..., :MLA_NOPE]
    q_rope = apply_rope(q[..., MLA_NOPE:], cos[:, None, :], sin[:, None, :])
    kpos = jnp.arange(s_len)
    scale = (MLA_NOPE + MLA_ROPE) ** -0.5

    def block(args):
        i, qn, qr = args
        sc = (jnp.einsum('bqhd,bkhd->bhqk', qn, k_nope, preferred_element_type=jnp.float32)
              + jnp.einsum('bqhr,bkr->bhqk', qr, k_rope, preferred_element_type=jnp.float32)) * scale
        qpos = i * QBLOCK + jnp.arange(QBLOCK)
        visible = kpos[None, :] < (qpos[:, None] // CHUNK + 1) * CHUNK
        sc = jnp.where(visible, sc, -jnp.inf)
        p = jax.nn.softmax(sc, axis=-1).astype(v.dtype)
        return jnp.einsum('bhqk,bkhd->bqhd', p, v)

    o = lax.map(block, (jnp.arange(s_len // QBLOCK), to_blocks(q_nope), to_blocks(q_rope)))
    return from_blocks(o).reshape(b, s_len, MLA_HEADS * MLA_V) @ w_out


def conv_ffn(h, w_in, conv_w, conv_b, w_out):
    s_len = h.shape[1]
    u = h @ w_in
    up = jnp.pad(u, ((0, 0), (CONV_W - 1, 0), (0, 0)))
    u = conv_b + sum(up[:, j:j + s_len] * conv_w[j] for j in range(CONV_W))
    gate, val = jnp.split(u, 2, axis=-1)
    return (jax.nn.silu(gate) * val) @ w_out


def setup_inputs(seed: int = 0) -> dict:
    key = jax.random.key(seed)
    ks = jax.random.split(key, 24)
    f32 = jnp.float32
    res = (2.0 * DEPTH) ** -0.5

    def nrm(k, shape, scale):
        return jax.random.normal(k, shape, f32) * scale

    def gain(k, shape):
        return 1.0 + 0.02 * jax.random.normal(k, shape, f32)

    w_fox_in = jnp.concatenate([
        nrm(ks[2], (N_A, D_MODEL, 3 * FOX_WIDTH), D_MODEL ** -0.5),
        nrm(ks[3], (N_A, D_MODEL, FOX_HEADS), 0.1 * D_MODEL ** -0.5)], axis=-1)
    return {
        'x': jax.random.normal(ks[0], (BATCH, SEQ, D_MODEL), f32),
        'fox_norm': gain(ks[1], (N_A, D_MODEL)),
        'w_fox_in': w_fox_in,
        'b_fox_f': jax.random.uniform(ks[4], (N_A, FOX_HEADS), f32, 1.0, 6.0),
        'w_fox_out': nrm(ks[5], (N_A, FOX_WIDTH, D_MODEL), FOX_WIDTH ** -0.5 * res),
        'kv_in_norm': gain(ks[6], (D_MODEL,)),
        'w_dkv': nrm(ks[7], (D_MODEL, KV_LORA + MLA_ROPE), D_MODEL ** -0.5),
        'kv_norm': gain(ks[8], (KV_LORA,)),
        'w_uk': nrm(ks[9], (KV_LORA, MLA_HEADS * MLA_NOPE), KV_LORA ** -0.5),
        'w_uv': nrm(ks[10], (KV_LORA, MLA_HEADS * MLA_V), KV_LORA ** -0.5),
        'mla_norm': gain(ks[11], (N_B, D_MODEL)),
        'w_dq': nrm(ks[12], (N_B, D_MODEL, Q_LORA), D_MODEL ** -0.5),
        'q_norm': gain(ks[13], (N_B, Q_LORA)),
        'w_uq': nrm(ks[14], (N_B, Q_LORA, MLA_HEADS * (MLA_NOPE + MLA_ROPE)), Q_LORA ** -0.5),
        'w_mla_out': nrm(ks[15], (N_B, MLA_HEADS * MLA_V, D_MODEL), (MLA_HEADS * MLA_V) ** -0.5 * res),
        'ffn_norm': gain(ks[16], (DEPTH, D_MODEL)),
        'w_ffn_in': nrm(ks[17], (DEPTH, D_MODEL, 2 * D_FF), D_MODEL ** -0.5),
        'ffn_conv_w': nrm(ks[18], (DEPTH, CONV_W, 2 * D_FF), CONV_W ** -0.5),
        'ffn_conv_b': nrm(ks[19], (DEPTH, 2 * D_FF), 0.01),
        'w_ffn_out': nrm(ks[20], (DEPTH, D_FF, D_MODEL), D_FF ** -0.5 * res),
        'final_norm': gain(ks[21], (D_MODEL,)),
    }


def reference(x, fox_norm, w_fox_in, b_fox_f, w_fox_out, kv_in_norm, w_dkv, kv_norm, w_uk, w_uv,
              mla_norm, w_dq, q_norm, w_uq, w_mla_out, ffn_norm, w_ffn_in, ffn_conv_w, ffn_conv_b,
              w_ffn_out, final_norm):
    cos, sin = rope_tables(x.shape[1], x.dtype)
    k_nope = k_rope = v = None
    for layer in range(DEPTH):
        if layer < N_A:
            x = x + fox_mixer(rmsnorm(x, fox_norm[layer]), w_fox_in[layer], b_fox_f[layer],
                              w_fox_out[layer])
        else:
            if layer == N_A:
                k_nope, k_rope, v = mla_shared_kv(x, kv_in_norm, w_dkv, kv_norm, w_uk, w_uv, cos, sin)
            j = layer - N_A
            x = x + mla_mixer(rmsnorm(x, mla_norm[j]), w_dq[j], q_norm[j], w_uq[j], w_mla_out[j],
                              k_nope, k_rope, v, cos, sin)
        x = x + conv_ffn(rmsnorm(x, ffn_norm[layer]), w_ffn_in[layer], ffn_conv_w[layer],
                         ffn_conv_b[layer], w_ffn_out[layer])
    return rmsnorm(x, final_norm)
```

```python
import functools
import math

import jax
import jax.numpy as jnp
from jax import lax
from jax.experimental import pallas as pl
from jax.experimental.pallas import tpu as pltpu

F32 = jnp.float32
BF16 = jnp.bfloat16

CHUNK = 64
FOX_HEADS = 16
FOX_HEAD_DIM = 64
MLA_HEADS = 16
MLA_NOPE = 128
MLA_ROPE = 64
MLA_V = 128
KV_LORA = 256
Q_LORA = 768
ROPE_THETA = 10000.0
D_FF = 2816
CONV_W = 3
EPS = 1e-6
LOG2E = math.log2(math.e)

LANES = 128
SEQ_TILE = 512
FF_TILE = 1408
FOX_DK = 128
FOX_DVA = 80
FOX_AUG = 16
MLA_DK = 256
MLA_DVA = 144
VMEM_LIMIT = 56 * 1024 * 1024
NEG_INIT = -1e30


def _dot(a, b):
    return jnp.dot(a, b, preferred_element_type=F32)


def _dot_nt(a, b):
    return lax.dot_general(a, b, (((1,), (1,)), ((), ())), preferred_element_type=F32)


def _dot_tn(a, b):
    return lax.dot_general(a, b, (((0,), (0,)), ((), ())), preferred_element_type=F32)


def _rms_scale(x):
    return lax.rsqrt(jnp.mean(x * x, axis=-1, keepdims=True) + EPS)


def _split3(x):
    hi = x.astype(BF16)
    r1 = x - hi.astype(F32)
    mid = r1.astype(BF16)
    r2 = r1 - mid.astype(F32)
    return hi, mid, r2.astype(BF16)


def _params(sem):
    return pltpu.CompilerParams(dimension_semantics=sem, vmem_limit_bytes=VMEM_LIMIT)


def _const_spec(shape):
    nd = len(shape)
    return pl.BlockSpec(shape, lambda *_: (0,) * nd, pipeline_mode=pl.Buffered(1))


def _fox_in_kernel(x_ref, g_ref, wqv_ref, wkf_ref, bf_ref, tri_ref, ksel_ref, qsel_ref,
                   qT_ref, k_ref, vT_ref, carry_ref, *, ts):
    heads, hd = FOX_HEADS, FOX_HEAD_DIM
    width = heads * hd

    @pl.when(pl.program_id(1) == 0)
    def _():
        carry_ref[...] = jnp.zeros_like(carry_ref)

    x = x_ref[0]
    h = (x * _rms_scale(x) * g_ref[...]).astype(BF16)
    projT = _dot_nt(wqv_ref[...], h)
    kf = _dot(h, wkf_ref[...])

    logf = jax.nn.log_sigmoid(kf[:, heads * LANES:] + bf_ref[...])
    tri = tri_ref[...]
    l_hi, l_mid, l_lo = _split3(logf)
    cum = (_dot(tri, l_hi) + _dot(tri, l_mid) + _dot(tri, l_lo)) + carry_ref[0:1, :]
    carry_ref[...] = jnp.broadcast_to(cum[ts - 1:ts, :], carry_ref.shape)
    c_hi, c_mid, c_lo = _split3(cum * LOG2E)

    lane = lax.broadcasted_iota(jnp.int32, (1, heads * LANES), 1) % LANES
    k_ones = jnp.where((lane >= hd + 3) & (lane < hd + 6), 1.0, 0.0).astype(F32)
    k_all = (kf[:, :heads * LANES] + _dot(c_hi, ksel_ref[0]) + _dot(c_mid, ksel_ref[1])
             + _dot(c_lo, ksel_ref[2]) + k_ones)
    for hh in range(heads):
        k_ref[0, hh, 0] = k_all[:, hh * LANES:(hh + 1) * LANES].astype(BF16)

    q_scale = (hd ** -0.5) * LOG2E
    qT_ref[0, :, 0, 0:hd, :] = (projT[:width] * q_scale).reshape(heads, hd, ts).astype(BF16)
    row = lax.broadcasted_iota(jnp.int32, (heads * FOX_AUG, 1), 0) % FOX_AUG
    augT = (_dot_nt(qsel_ref[0], c_hi) + _dot_nt(qsel_ref[1], c_mid) + _dot_nt(qsel_ref[2], c_lo)
            + jnp.where(row < 3, 1.0, 0.0).astype(F32))
    qT_ref[0, :, 0, hd:hd + FOX_AUG, :] = augT.reshape(heads, FOX_AUG, ts).astype(BF16)
    qT_ref[0, :, 0, hd + FOX_AUG:, :] = jnp.zeros((heads, FOX_DK - hd - FOX_AUG, ts), BF16)

    vT_ref[0, :, 0, 0:hd, :] = projT[width:].reshape(heads, hd, ts).astype(BF16)
    vrow = lax.broadcasted_iota(jnp.int32, (heads, FOX_DVA - hd, ts), 1)
    vT_ref[0, :, 0, hd:, :] = jnp.where(vrow == 0, 1.0, 0.0).astype(BF16)


def _fox_in(x, g, w_in, b_f):
    bsz, seq, d = x.shape
    ts = SEQ_TILE
    nt = seq // ts
    heads, hd = FOX_HEADS, FOX_HEAD_DIM
    width = heads * hd
    wqvT = jnp.concatenate([w_in[:, :width], w_in[:, 2 * width:3 * width]], axis=1).T.astype(BF16)
    wk = w_in[:, width:2 * width].reshape(d, heads, hd)
    wk = jnp.pad(wk, ((0, 0), (0, 0), (0, LANES - hd))).reshape(d, heads * LANES)
    wf = jnp.pad(w_in[:, 3 * width:], ((0, 0), (0, LANES - heads)))
    wkf = jnp.concatenate([wk, wf], axis=1).astype(BF16)
    bf = jnp.pad(b_f, (0, LANES - heads)).reshape(1, LANES)
    tri = jnp.tril(jnp.ones((ts, ts), F32)).astype(BF16)
    hidx = jnp.arange(heads)
    ksel = jnp.zeros((3, LANES, heads * LANES), F32)
    qsel = jnp.zeros((3, heads * FOX_AUG, LANES), F32)
    for i in range(3):
        ksel = ksel.at[i, hidx, hidx * LANES + hd + i].set(-1.0)
        qsel = qsel.at[i, hidx * FOX_AUG + 3 + i, hidx].set(1.0)
    ksel, qsel = ksel.astype(BF16), qsel.astype(BF16)

    kern = functools.partial(_fox_in_kernel, ts=ts)
    return pl.pallas_call(
        kern,
        grid=(bsz, nt),
        in_specs=[
            pl.BlockSpec((1, ts, d), lambda b, s: (b, s, 0)),
            _const_spec((1, d)),
            _const_spec(wqvT.shape),
            _const_spec(wkf.shape),
            _const_spec(bf.shape),
            _const_spec(tri.shape),
            _const_spec(ksel.shape),
            _const_spec(qsel.shape),
        ],
        out_specs=[
            pl.BlockSpec((1, heads, 1, FOX_DK, ts), lambda b, s: (b, 0, s, 0, 0)),
            pl.BlockSpec((1, heads, 1, ts, FOX_DK), lambda b, s: (b, 0, s, 0, 0)),
            pl.BlockSpec((1, heads, 1, FOX_DVA, ts), lambda b, s: (b, 0, s, 0, 0)),
        ],
        out_shape=[
            jax.ShapeDtypeStruct((bsz, heads, nt, FOX_DK, ts), BF16),
            jax.ShapeDtypeStruct((bsz, heads, nt, ts, FOX_DK), BF16),
            jax.ShapeDtypeStruct((bsz, heads, nt, FOX_DVA, ts), BF16),
        ],
        scratch_shapes=[pltpu.VMEM((8, LANES), F32)],
        compiler_params=_params(("arbitrary", "arbitrary")),
        name="fox_in",
    )(x, g.reshape(1, d), wqvT, wkf, bf, tri, ksel, qsel)


def _flash_kernel(q_ref, k_ref, v_ref, o_ref, m_ref, acc_ref, *, nt, t, dv, chunk):
    def step(qi, ki, diagonal):
        s = _dot(k_ref[0, 0, ki], q_ref[0, 0, qi])
        if diagonal:
            krow = lax.broadcasted_iota(jnp.int32, (t, t), 0)
            qcol = lax.broadcasted_iota(jnp.int32, (t, t), 1)
            if chunk > 1:
                visible = (krow // chunk) <= (qcol // chunk)
            else:
                visible = krow <= qcol
            s = jnp.where(visible, s, -jnp.inf)
        m_prev = m_ref[...]
        m_new = jnp.maximum(m_prev, jnp.max(s, axis=0, keepdims=True))
        p = jnp.exp2(s - m_new).astype(BF16)
        alpha = jnp.exp2(m_prev - m_new)
        acc_ref[...] = alpha * acc_ref[...] + _dot(v_ref[0, 0, ki], p)
        m_ref[...] = m_new

    def q_body(qi, carry):
        m_ref[...] = jnp.full(m_ref.shape, NEG_INIT, F32)
        acc_ref[...] = jnp.zeros_like(acc_ref)

        def kv_body(ki, c):
            step(qi, ki, False)
            return c

        lax.fori_loop(0, qi, kv_body, 0)
        step(qi, qi, True)
        acc = acc_ref[...]
        o_ref[0, 0, qi] = (acc[:dv] / acc[dv:dv + 1]).astype(BF16)
        return carry

    lax.fori_loop(0, nt, q_body, 0)


def _flash(qT, k, vT, *, dv, chunk, name):
    bsz, heads, nt, dk, t = qT.shape
    dva = vT.shape[3]
    kern = functools.partial(_flash_kernel, nt=nt, t=t, dv=dv, chunk=chunk)
    return pl.pallas_call(
        kern,
        grid=(bsz, heads),
        in_specs=[
            pl.BlockSpec((1, 1, nt, dk, t), lambda b, h: (b, h, 0, 0, 0)),
            pl.BlockSpec((1, 1, nt, t, dk), lambda b, h: (b, h, 0, 0, 0)),
            pl.BlockSpec((1, 1, nt, dva, t), lambda b, h: (b, h, 0, 0, 0)),
        ],
        out_specs=pl.BlockSpec((1, 1, nt, dv, t), lambda b, h: (b, h, 0, 0, 0)),
        out_shape=jax.ShapeDtypeStruct((bsz, heads, nt, dv, t), BF16),
        scratch_shapes=[pltpu.VMEM((1, t), F32), pltpu.VMEM((dva, t), F32)],
        compiler_params=_params(("arbitrary", "arbitrary")),
        name=name,
    )(qT, k, vT)


def _causal_taps(u, prev):
    r1 = pltpu.roll(u, 1, axis=0)
    r2 = pltpu.roll(u, 2, axis=0)
    rows = lax.broadcasted_iota(jnp.int32, (8, u.shape[1]), 0)
    top1 = jnp.where(rows == 0, prev[7:8], r1[:8])
    top2 = jnp.where(rows == 0, prev[6:7], jnp.where(rows == 1, prev[7:8], r2[:8]))
    return (jnp.concatenate([top1, r1[8:]], axis=0), jnp.concatenate([top2, r2[8:]], axis=0))


def _ffn_kernel(x_ref, oT_ref, wo_ref, g_ref, win_ref, cw_ref, cb_ref, wout_ref, fg_ref,
                out_ref, carry_ref, *, ts, final_norm):
    @pl.when(pl.program_id(1) == 0)
    def _():
        carry_ref[...] = jnp.zeros_like(carry_ref)

    heads, dv = oT_ref.shape[1], oT_ref.shape[3]
    oT = oT_ref[0, :, 0].reshape(heads * dv, ts)
    x1 = x_ref[0] + _dot_tn(oT, wo_ref[...])
    h = (x1 * _rms_scale(x1) * g_ref[...]).astype(BF16)

    def conv_branch(col0):
        cols = pl.ds(col0, FF_TILE)
        u = _dot(h, win_ref[:, cols])
        prev = carry_ref[:, cols]
        carry_ref[:, cols] = u[ts - 8:]
        u1, u2 = _causal_taps(u, prev)
        cw = cw_ref[:, cols]
        return cb_ref[:, cols] + (u2 * cw[0:1] + u1 * cw[1:2] + u * cw[2:3])

    acc = jnp.zeros_like(x1)
    for c in range(D_FF // FF_TILE):
        gate = conv_branch(c * FF_TILE)
        val = conv_branch(D_FF + c * FF_TILE)
        act = (jax.nn.silu(gate) * val).astype(BF16)
        acc = acc + _dot(act, wout_ref[pl.ds(c * FF_TILE, FF_TILE), :])
    x2 = x1 + acc
    if final_norm:
        x2 = x2 * _rms_scale(x2) * fg_ref[...]
    out_ref[0] = x2


def _attn_out_ffn(x, oT, w_o, g, w_in, conv_w, conv_b, w_out, final_g, *, final_norm):
    bsz, seq, d = x.shape
    ts = SEQ_TILE
    nt = seq // ts
    _, heads, _, dv, t = oT.shape
    assert t == ts
    kern = functools.partial(_ffn_kernel, ts=ts, final_norm=final_norm)
    return pl.pallas_call(
        kern,
        grid=(bsz, nt),
        in_specs=[
            pl.BlockSpec((1, ts, d), lambda b, s: (b, s, 0)),
            pl.BlockSpec((1, heads, 1, dv, ts), lambda b, s: (b, 0, s, 0, 0)),
            _const_spec((heads * dv, d)),
            _const_spec((1, d)),
            _const_spec((d, 2 * D_FF)),
            _const_spec((CONV_W, 2 * D_FF)),
            _const_spec((1, 2 * D_FF)),
            _const_spec((D_FF, d)),
            _const_spec((1, d)),
        ],
        out_specs=pl.BlockSpec((1, ts, d), lambda b, s: (b, s, 0)),
        out_shape=jax.ShapeDtypeStruct((bsz, seq, d), F32),
        scratch_shapes=[pltpu.VMEM((8, 2 * D_FF), F32)],
        compiler_params=_params(("arbitrary", "arbitrary")),
        name="attn_out_ffn",
    )(x, oT, w_o.astype(BF16), g.reshape(1, d), w_in.astype(BF16), conv_w,
      conv_b.reshape(1, 2 * D_FF), w_out.astype(BF16), final_g.reshape(1, d))


def _mla_in_kernel(x_ref, gkv_ref, gq_ref, wd_ref, kvn_ref, kcos_ref, ksin_ref, wuk_ref, wuvT_ref,
                   wdq_ref, qn_ref, wuqT_ref, cosT_ref, sinT_ref, qT_ref, k_ref, vT_ref, *, ts):
    heads = MLA_HEADS
    half = MLA_ROPE // 2
    x = x_ref[0]
    xn = x * _rms_scale(x)
    hkv = (xn * gkv_ref[...]).astype(BF16)
    hq = (xn * gq_ref[...]).astype(BF16)

    ckv = _dot(hkv, wd_ref[...])
    lat = ckv[:, :KV_LORA]
    latent = (lat * _rms_scale(lat) * kvn_ref[...]).astype(BF16)
    k_rope = (ckv[:, KV_LORA:KV_LORA + LANES] * kcos_ref[...]
              + ckv[:, KV_LORA + LANES:] * ksin_ref[...]).astype(BF16)
    k_nope = _dot(latent, wuk_ref[...])
    for hh in range(heads):
        k_ref[0, hh, 0, :, 0:MLA_NOPE] = k_nope[:, hh * MLA_NOPE:(hh + 1) * MLA_NOPE].astype(BF16)
        k_ref[0, hh, 0, :, MLA_NOPE:] = k_rope
    vT = _dot_nt(wuvT_ref[...], latent)
    vT_ref[0, :, 0, 0:MLA_V, :] = vT.reshape(heads, MLA_V, ts).astype(BF16)
    vrow = lax.broadcasted_iota(jnp.int32, (heads, MLA_DVA - MLA_V, ts), 1)
    vT_ref[0, :, 0, MLA_V:, :] = jnp.where(vrow == 0, 1.0, 0.0).astype(BF16)

    ql = _dot(hq, wdq_ref[...])
    qlat = (ql * _rms_scale(ql) * qn_ref[...]).astype(BF16)
    qT = _dot_nt(wuqT_ref[...], qlat).reshape(heads, MLA_NOPE + MLA_ROPE, ts)
    q_scale = ((MLA_NOPE + MLA_ROPE) ** -0.5) * LOG2E
    x1 = qT[:, MLA_NOPE:MLA_NOPE + half]
    x2 = qT[:, MLA_NOPE + half:]
    cos = cosT_ref[...][None]
    sin = sinT_ref[...][None]
    qT_ref[0, :, 0, 0:MLA_NOPE, :] = (qT[:, :MLA_NOPE] * q_scale).astype(BF16)
    qT_ref[0, :, 0, MLA_NOPE:MLA_NOPE + half, :] = ((x1 * cos - x2 * sin) * q_scale).astype(BF16)
    qT_ref[0, :, 0, MLA_NOPE + half:MLA_NOPE + MLA_ROPE, :] = ((x1 * sin + x2 * cos) * q_scale).astype(BF16)
    qT_ref[0, :, 0, MLA_NOPE + MLA_ROPE:, :] = jnp.zeros((heads, MLA_DK - MLA_NOPE - MLA_ROPE, ts), BF16)


def _mla_in(x, kv_in_g, q_in_g, w_dkv, kv_norm, w_uk, w_uv, w_dq, q_norm, w_uq):
    bsz, seq, d = x.shape
    ts = SEQ_TILE
    nt = seq // ts
    heads = MLA_HEADS
    half = MLA_ROPE // 2

    pos = jnp.arange(seq, dtype=F32)
    inv = ROPE_THETA ** (-jnp.arange(0, MLA_ROPE, 2, dtype=F32) / MLA_ROPE)
    ang = pos[:, None] * inv[None, :]
    cos, sin = jnp.cos(ang), jnp.sin(ang)
    zpad = jnp.zeros((seq, LANES - MLA_ROPE), F32)
    kcos = jnp.concatenate([cos, cos, zpad], axis=1)
    ksin = jnp.concatenate([-sin, sin, zpad], axis=1)
    cosT, sinT = cos.T, sin.T

    zcol = jnp.zeros((d, LANES - MLA_ROPE), F32)
    wd = jnp.concatenate([
        w_dkv, zcol,
        w_dkv[:, KV_LORA + half:], w_dkv[:, KV_LORA:KV_LORA + half], zcol], axis=1).astype(BF16)

    kern = functools.partial(_mla_in_kernel, ts=ts)
    return pl.pallas_call(
        kern,
        grid=(bsz, nt),
        in_specs=[
            pl.BlockSpec((1, ts, d), lambda b, s: (b, s, 0)),
            _const_spec((1, d)),
            _const_spec((1, d)),
            _const_spec(wd.shape),
            _const_spec((1, KV_LORA)),
            pl.BlockSpec((ts, LANES), lambda b, s: (s, 0)),
            pl.BlockSpec((ts, LANES), lambda b, s: (s, 0)),
            _const_spec((KV_LORA, heads * MLA_NOPE)),
            _const_spec((heads * MLA_V, KV_LORA)),
            _const_spec((d, Q_LORA)),
            _const_spec((1, Q_LORA)),
            _const_spec((heads * (MLA_NOPE + MLA_ROPE), Q_LORA)),
            pl.BlockSpec((half, ts), lambda b, s: (0, s)),
            pl.BlockSpec((half, ts), lambda b, s: (0, s)),
        ],
        out_specs=[
            pl.BlockSpec((1, heads, 1, MLA_DK, ts), lambda b, s: (b, 0, s, 0, 0)),
            pl.BlockSpec((1, heads, 1, ts, MLA_DK), lambda b, s: (b, 0, s, 0, 0)),
            pl.BlockSpec((1, heads, 1, MLA_DVA, ts), lambda b, s: (b, 0, s, 0, 0)),
        ],
        out_shape=[
            jax.ShapeDtypeStruct((bsz, heads, nt, MLA_DK, ts), BF16),
            jax.ShapeDtypeStruct((bsz, heads, nt, ts, MLA_DK), BF16),
            jax.ShapeDtypeStruct((bsz, heads, nt, MLA_DVA, ts), BF16),
        ],
        compiler_params=_params(("arbitrary", "arbitrary")),
        name="mla_in",
    )(x, kv_in_g.reshape(1, d), q_in_g.reshape(1, d), wd, kv_norm.reshape(1, KV_LORA), kcos, ksin,
      w_uk.astype(BF16), w_uv.T.astype(BF16), w_dq.astype(BF16), q_norm.reshape(1, Q_LORA),
      w_uq.T.astype(BF16), cosT, sinT)


def kernel(x, fox_norm, w_fox_in, b_fox_f, w_fox_out, kv_in_norm, w_dkv, kv_norm, w_uk, w_uv, mla_norm, w_dq, q_norm, w_uq, w_mla_out, ffn_norm, w_ffn_in, ffn_conv_w, ffn_conv_b, w_ffn_out, final_norm):
    depth = ffn_norm.shape[0]
    n_a = fox_norm.shape[0]
    kv = None
    for layer in range(depth):
        last = layer == depth - 1
        if layer < n_a:
            qT, k, vT = _fox_in(x, fox_norm[layer], w_fox_in[layer], b_fox_f[layer])
            oT = _flash(qT, k, vT, dv=FOX_HEAD_DIM, chunk=1, name="fox_attn")
            w_o = w_fox_out[layer]
        else:
            j = layer - n_a
            qT, k_new, vT_new = _mla_in(x, kv_in_norm, mla_norm[j], w_dkv, kv_norm, w_uk, w_uv,
                                        w_dq[j], q_norm[j], w_uq[j])
            if kv is None:
                kv = (k_new, vT_new)
            oT = _flash(qT, kv[0], kv[1], dv=MLA_V, chunk=CHUNK, name="mla_attn")
            w_o = w_mla_out[j]
        x = _attn_out_ffn(x, oT, w_o, ffn_norm[layer], w_ffn_in[layer], ffn_conv_w[layer],
                          ffn_conv_b[layer], w_ffn_out[layer], final_norm, final_norm=last)
    return x
```

```python
import functools
import math

import jax
import jax.numpy as jnp
from jax import lax
from jax.experimental import pallas as pl
from jax.experimental.pallas import tpu as pltpu

F32 = jnp.float32
BF16 = jnp.bfloat16

CHUNK = 64
FOX_HEADS = 16
FOX_HEAD_DIM = 64
MLA_HEADS = 16
MLA_NOPE = 128
MLA_ROPE = 64
MLA_V = 128
KV_LORA = 256
Q_LORA = 768
ROPE_THETA = 10000.0
D_FF = 2816
CONV_W = 3
EPS = 1e-6
LOG2E = math.log2(math.e)

LANES = 128
SEQ_TILE = 512
FF_TILE = 1408
FOX_DK = 128
FOX_DVA = 80
FOX_AUG = 16
MLA_DK = 256
MLA_DVA = 144
VMEM_LIMIT = 56 * 1024 * 1024
FLASH_FLAGS = None
PIPE_LAG_B = 2
PIPE_LAG_C = 4
PIPE_BUFS = 3


def _dot(a, b):
    return jnp.dot(a, b, preferred_element_type=F32)


def _dot_nt(a, b):
    return lax.dot_general(a, b, (((1,), (1,)), ((), ())), preferred_element_type=F32)


def _dot_tn(a, b):
    return lax.dot_general(a, b, (((0,), (0,)), ((), ())), preferred_element_type=F32)


def _rms_scale(x):
    return lax.rsqrt(jnp.mean(x * x, axis=-1, keepdims=True) + EPS)


def _split3(x):
    hi = x.astype(BF16)
    r1 = x - hi.astype(F32)
    mid = r1.astype(BF16)
    r2 = r1 - mid.astype(F32)
    return hi, mid, r2.astype(BF16)


def _params(sem, flags=None):
    return pltpu.CompilerParams(dimension_semantics=sem, vmem_limit_bytes=VMEM_LIMIT, flags=flags)


def _const_spec(shape):
    nd = len(shape)
    return pl.BlockSpec(shape, lambda *_: (0,) * nd, pipeline_mode=pl.Buffered(1))


def _fox_in_kernel(x_ref, g_ref, wqv_ref, wkf_ref, bf_ref, tri_ref, ksel_ref, qsel_ref,
                   qT_ref, k_ref, vT_ref, carry_ref, *, ts):
    heads, hd = FOX_HEADS, FOX_HEAD_DIM
    width = heads * hd

    @pl.when(pl.program_id(1) == 0)
    def _():
        carry_ref[...] = jnp.zeros_like(carry_ref)

    x = x_ref[0]
    h = (x * _rms_scale(x) * g_ref[...]).astype(BF16)
    projT = _dot_nt(wqv_ref[...], h)
    kf = _dot(h, wkf_ref[...])

    logf = jax.nn.log_sigmoid(kf[:, heads * LANES:] + bf_ref[...])
    tri = tri_ref[...]
    l_hi, l_mid, l_lo = _split3(logf)
    cum = (_dot(tri, l_hi) + _dot(tri, l_mid) + _dot(tri, l_lo)) + carry_ref[0:1, :]
    carry_ref[...] = jnp.broadcast_to(cum[ts - 1:ts, :], carry_ref.shape)
    c_hi, c_mid, c_lo = _split3(cum * LOG2E)

    lane = lax.broadcasted_iota(jnp.int32, (1, heads * LANES), 1) % LANES
    k_ones = jnp.where((lane >= hd + 3) & (lane < hd + 6), 1.0, 0.0).astype(F32)
    k_all = (kf[:, :heads * LANES] + _dot(c_hi, ksel_ref[0]) + _dot(c_mid, ksel_ref[1])
             + _dot(c_lo, ksel_ref[2]) + k_ones)
    for hh in range(heads):
        k_ref[0, hh, 0] = k_all[:, hh * LANES:(hh + 1) * LANES].astype(BF16)

    q_scale = (hd ** -0.5) * LOG2E
    qT_ref[0, :, 0, 0:hd, :] = (projT[:width] * q_scale).reshape(heads, hd, ts).astype(BF16)
    row = lax.broadcasted_iota(jnp.int32, (heads * FOX_AUG, 1), 0) % FOX_AUG
    augT = (_dot_nt(qsel_ref[0], c_hi) + _dot_nt(qsel_ref[1], c_mid) + _dot_nt(qsel_ref[2], c_lo)
            + jnp.where(row < 3, 1.0, 0.0).astype(F32))
    qT_ref[0, :, 0, hd:hd + FOX_AUG, :] = augT.reshape(heads, FOX_AUG, ts).astype(BF16)
    qT_ref[0, :, 0, hd + FOX_AUG:, :] = jnp.zeros((heads, FOX_DK - hd - FOX_AUG, ts), BF16)

    vT_ref[0, :, 0, 0:hd, :] = projT[width:].reshape(heads, hd, ts).astype(BF16)
    vrow = lax.broadcasted_iota(jnp.int32, (heads, FOX_DVA - hd, ts), 1)
    vT_ref[0, :, 0, hd:, :] = jnp.where(vrow == 0, 1.0, 0.0).astype(BF16)


def _fox_in(x, g, w_in, b_f):
    bsz, seq, d = x.shape
    ts = SEQ_TILE
    nt = seq // ts
    heads, hd = FOX_HEADS, FOX_HEAD_DIM
    width = heads * hd
    wqvT = jnp.concatenate([w_in[:, :width], w_in[:, 2 * width:3 * width]], axis=1).T.astype(BF16)
    wk = w_in[:, width:2 * width].reshape(d, heads, hd)
    wk = jnp.pad(wk, ((0, 0), (0, 0), (0, LANES - hd))).reshape(d, heads * LANES)
    wf = jnp.pad(w_in[:, 3 * width:], ((0, 0), (0, LANES - heads)))
    wkf = jnp.concatenate([wk, wf], axis=1).astype(BF16)
    bf = jnp.pad(b_f, (0, LANES - heads)).reshape(1, LANES)
    tri = jnp.tril(jnp.ones((ts, ts), F32)).astype(BF16)
    hidx = jnp.arange(heads)
    ksel = jnp.zeros((3, LANES, heads * LANES), F32)
    qsel = jnp.zeros((3, heads * FOX_AUG, LANES), F32)
    for i in range(3):
        ksel = ksel.at[i, hidx, hidx * LANES + hd + i].set(-1.0)
        qsel = qsel.at[i, hidx * FOX_AUG + 3 + i, hidx].set(1.0)
    ksel, qsel = ksel.astype(BF16), qsel.astype(BF16)

    kern = functools.partial(_fox_in_kernel, ts=ts)
    return pl.pallas_call(
        kern,
        grid=(bsz, nt),
        in_specs=[
            pl.BlockSpec((1, ts, d), lambda b, s: (b, s, 0)),
            _const_spec((1, d)),
            _const_spec(wqvT.shape),
            _const_spec(wkf.shape),
            _const_spec(bf.shape),
            _const_spec(tri.shape),
            _const_spec(ksel.shape),
            _const_spec(qsel.shape),
        ],
        out_specs=[
            pl.BlockSpec((1, heads, 1, FOX_DK, ts), lambda b, s: (b, 0, s, 0, 0)),
            pl.BlockSpec((1, heads, 1, ts, FOX_DK), lambda b, s: (b, 0, s, 0, 0)),
            pl.BlockSpec((1, heads, 1, FOX_DVA, ts), lambda b, s: (b, 0, s, 0, 0)),
        ],
        out_shape=[
            jax.ShapeDtypeStruct((bsz, heads, nt, FOX_DK, ts), BF16),
            jax.ShapeDtypeStruct((bsz, heads, nt, ts, FOX_DK), BF16),
            jax.ShapeDtypeStruct((bsz, heads, nt, FOX_DVA, ts), BF16),
        ],
        scratch_shapes=[pltpu.VMEM((8, LANES), F32)],
        compiler_params=_params(("arbitrary", "arbitrary")),
        name="fox_in",
    )(x, g.reshape(1, d), wqvT, wkf, bf, tri, ksel, qsel)


def _flash_kernel(qi_tab, ki_tab, q_ref, k_ref, v_ref, o_ref, s0_ref, s1_ref, s2_ref,
                  p0_ref, p1_ref, p2_ref, m_ref, acc_ref, *, nt, t, dv, chunk):
    s_bufs = (s0_ref, s1_ref, s2_ref)
    p_bufs = (p0_ref, p1_ref, p2_ref)
    nbuf = len(s_bufs)
    assert PIPE_LAG_B < nbuf and PIPE_LAG_C - PIPE_LAG_B < nbuf

    def run(n, tile_ids, diagonal):
        def stage_a(i, slot):
            qi, ki = tile_ids(i)
            s = _dot(k_ref[0, 0, ki], q_ref[0, 0, qi])
            if diagonal:
                krow = lax.broadcasted_iota(jnp.int32, (t, t), 0)
                qcol = lax.broadcasted_iota(jnp.int32, (t, t), 1)
                if chunk > 1:
                    visible = (krow // chunk) <= (qcol // chunk)
                else:
                    visible = krow <= qcol
                s = jnp.where(visible, s, -jnp.inf)
            s_bufs[slot][...] = s
            m_new = jnp.max(s, axis=0, keepdims=True)
            if diagonal:
                alpha = m_new
            else:
                m_old = m_ref[qi]
                m_new = jnp.maximum(m_old, m_new)
                alpha = jnp.exp2(m_old - m_new)
            m_ref[qi] = m_new
            return m_new, alpha

        def stage_b(slot, m_new):
            p_bufs[slot][...] = jnp.exp2(s_bufs[slot][...] - m_new).astype(BF16)

        def stage_c(i, slot, alpha):
            qi, ki = tile_ids(i)
            pv = _dot(v_ref[0, 0, ki], p_bufs[slot][...])
            if diagonal:
                acc_ref[qi] = pv
            else:
                acc_ref[qi] = alpha * acc_ref[qi] + pv

        def iteration(it, slot_a, hist, do_a=True, do_b=True, do_c=True):
            if do_b:
                stage_b((slot_a - PIPE_LAG_B) % nbuf, hist[PIPE_LAG_B - 1][0])
            if do_c:
                stage_c(it - PIPE_LAG_C, (slot_a - PIPE_LAG_C) % nbuf, hist[PIPE_LAG_C - 1][1])
            new = stage_a(it, slot_a) if do_a else None
            return [new] + hist[:-1]

        def static_iteration(it, hist):
            return iteration(it, it % nbuf, hist, do_a=it < n, do_b=0 <= it - PIPE_LAG_B < n,
                             do_c=0 <= it - PIPE_LAG_C < n)

        n_steady = max(n - PIPE_LAG_C, 0)
        first = PIPE_LAG_C + n_steady % nbuf
        hist = [None] * PIPE_LAG_C
        for it in range(first):
            hist = static_iteration(it, hist)

        if n_steady // nbuf:
            def group(j, flat):
                h = [(flat[2 * k], flat[2 * k + 1]) for k in range(PIPE_LAG_C)]
                for e in range(nbuf):
                    h = iteration(first + nbuf * j + e, (first + e) % nbuf, h)
                return tuple(x for pair in h for x in pair)

            flat = lax.fori_loop(0, n_steady // nbuf, group, tuple(x for pair in hist for x in pair))
            hist = [(flat[2 * k], flat[2 * k + 1]) for k in range(PIPE_LAG_C)]
        for it in range(first + nbuf * (n_steady // nbuf), n + PIPE_LAG_C):
            hist = static_iteration(it, hist)

    run(nt, lambda i: (i, i), True)
    run(nt * (nt - 1) // 2, lambda i: (qi_tab[i], ki_tab[i]), False)

    def finalize(qi, carry):
        acc = acc_ref[qi]
        o_ref[0, 0, qi] = (acc[:dv] / acc[dv:dv + 1]).astype(BF16)
        return carry

    lax.fori_loop(0, nt, finalize, 0)


def _flash(qT, k, vT, *, dv, chunk, name):
    bsz, heads, nt, dk, t = qT.shape
    dva = vT.shape[3]
    pairs = [(qi, ki) for qi in range(nt) for ki in range(qi)]
    qi_tab = jnp.asarray([p[0] for p in pairs], jnp.int32)
    ki_tab = jnp.asarray([p[1] for p in pairs], jnp.int32)
    kern = functools.partial(_flash_kernel, nt=nt, t=t, dv=dv, chunk=chunk)
    grid_spec = pltpu.PrefetchScalarGridSpec(
        num_scalar_prefetch=2,
        grid=(bsz, heads),
        in_specs=[
            pl.BlockSpec((1, 1, nt, dk, t), lambda b, h, *_: (b, h, 0, 0, 0)),
            pl.BlockSpec((1, 1, nt, t, dk), lambda b, h, *_: (b, h, 0, 0, 0)),
            pl.BlockSpec((1, 1, nt, dva, t), lambda b, h, *_: (b, h, 0, 0, 0)),
        ],
        out_specs=pl.BlockSpec((1, 1, nt, dv, t), lambda b, h, *_: (b, h, 0, 0, 0)),
        scratch_shapes=(
            [pltpu.VMEM((t, t), F32)] * PIPE_BUFS + [pltpu.VMEM((t, t), BF16)] * PIPE_BUFS
            + [pltpu.VMEM((nt, 1, t), F32), pltpu.VMEM((nt, dva, t), F32)]),
    )
    return pl.pallas_call(
        kern,
        grid_spec=grid_spec,
        out_shape=jax.ShapeDtypeStruct((bsz, heads, nt, dv, t), BF16),
        compiler_params=_params(("arbitrary", "arbitrary"), FLASH_FLAGS),
        name=name,
    )(qi_tab, ki_tab, qT, k, vT)


def _causal_taps(u, prev):
    r1 = pltpu.roll(u, 1, axis=0)
    r2 = pltpu.roll(u, 2, axis=0)
    rows = lax.broadcasted_iota(jnp.int32, (8, u.shape[1]), 0)
    top1 = jnp.where(rows == 0, prev[7:8], r1[:8])
    top2 = jnp.where(rows == 0, prev[6:7], jnp.where(rows == 1, prev[7:8], r2[:8]))
    return (jnp.concatenate([top1, r1[8:]], axis=0), jnp.concatenate([top2, r2[8:]], axis=0))


def _ffn_kernel(x_ref, oT_ref, wo_ref, g_ref, win_ref, cw_ref, cb_ref, wout_ref, fg_ref,
                out_ref, carry_ref, *, ts, final_norm):
    @pl.when(pl.program_id(1) == 0)
    def _():
        carry_ref[...] = jnp.zeros_like(carry_ref)

    heads, dv = oT_ref.shape[1], oT_ref.shape[3]
    oT = oT_ref[0, :, 0].reshape(heads * dv, ts)
    x1 = x_ref[0] + _dot_tn(oT, wo_ref[...])
    h = (x1 * _rms_scale(x1) * g_ref[...]).astype(BF16)

    def conv_branch(col0):
        cols = pl.ds(col0, FF_TILE)
        u = _dot(h, win_ref[:, cols])
        prev = carry_ref[:, cols]
        carry_ref[:, cols] = u[ts - 8:]
        u1, u2 = _causal_taps(u, prev)
        cw = cw_ref[:, cols]
        return cb_ref[:, cols] + (u2 * cw[0:1] + u1 * cw[1:2] + u * cw[2:3])

    acc = jnp.zeros_like(x1)
    for c in range(D_FF // FF_TILE):
        gate = conv_branch(c * FF_TILE)
        val = conv_branch(D_FF + c * FF_TILE)
        act = (jax.nn.silu(gate) * val).astype(BF16)
        acc = acc + _dot(act, wout_ref[pl.ds(c * FF_TILE, FF_TILE), :])
    x2 = x1 + acc
    if final_norm:
        x2 = x2 * _rms_scale(x2) * fg_ref[...]
    out_ref[0] = x2


def _attn_out_ffn(x, oT, w_o, g, w_in, conv_w, conv_b, w_out, final_g, *, final_norm):
    bsz, seq, d = x.shape
    ts = SEQ_TILE
    nt = seq // ts
    _, heads, _, dv, t = oT.shape
    assert t == ts
    kern = functools.partial(_ffn_kernel, ts=ts, final_norm=final_norm)
    return pl.pallas_call(
        kern,
        grid=(bsz, nt),
        in_specs=[
            pl.BlockSpec((1, ts, d), lambda b, s: (b, s, 0)),
            pl.BlockSpec((1, heads, 1, dv, ts), lambda b, s: (b, 0, s, 0, 0)),
            _const_spec((heads * dv, d)),
            _const_spec((1, d)),
            _const_spec((d, 2 * D_FF)),
            _const_spec((CONV_W, 2 * D_FF)),
            _const_spec((1, 2 * D_FF)),
            _const_spec((D_FF, d)),
            _const_spec((1, d)),
        ],
        out_specs=pl.BlockSpec((1, ts, d), lambda b, s: (b, s, 0)),
        out_shape=jax.ShapeDtypeStruct((bsz, seq, d), F32),
        scratch_shapes=[pltpu.VMEM((8, 2 * D_FF), F32)],
        compiler_params=_params(("arbitrary", "arbitrary")),
        name="attn_out_ffn",
    )(x, oT, w_o.astype(BF16), g.reshape(1, d), w_in.astype(BF16), conv_w,
      conv_b.reshape(1, 2 * D_FF), w_out.astype(BF16), final_g.reshape(1, d))


def _mla_in_kernel(x_ref, gkv_ref, gq_ref, wd_ref, kvn_ref, kcos_ref, ksin_ref, wuk_ref, wuvT_ref,
                   wdq_ref, qn_ref, wuqT_ref, cosT_ref, sinT_ref, qT_ref, k_ref, vT_ref, *, ts):
    heads = MLA_HEADS
    half = MLA_ROPE // 2
    x = x_ref[0]
    xn = x * _rms_scale(x)
    hkv = (xn * gkv_ref[...]).astype(BF16)
    hq = (xn * gq_ref[...]).astype(BF16)

    ckv = _dot(hkv, wd_ref[...])
    lat = ckv[:, :KV_LORA]
    latent = (lat * _rms_scale(lat) * kvn_ref[...]).astype(BF16)
    k_rope = (ckv[:, KV_LORA:KV_LORA + LANES] * kcos_ref[...]
              + ckv[:, KV_LORA + LANES:] * ksin_ref[...]).astype(BF16)
    k_nope = _dot(latent, wuk_ref[...])
    for hh in range(heads):
        k_ref[0, hh, 0, :, 0:MLA_NOPE] = k_nope[:, hh * MLA_NOPE:(hh + 1) * MLA_NOPE].astype(BF16)
        k_ref[0, hh, 0, :, MLA_NOPE:] = k_rope
    vT = _dot_nt(wuvT_ref[...], latent)
    vT_ref[0, :, 0, 0:MLA_V, :] = vT.reshape(heads, MLA_V, ts).astype(BF16)
    vrow = lax.broadcasted_iota(jnp.int32, (heads, MLA_DVA - MLA_V, ts), 1)
    vT_ref[0, :, 0, MLA_V:, :] = jnp.where(vrow == 0, 1.0, 0.0).astype(BF16)

    ql = _dot(hq, wdq_ref[...])
    qlat = (ql * _rms_scale(ql) * qn_ref[...]).astype(BF16)
    qT = _dot_nt(wuqT_ref[...], qlat).reshape(heads, MLA_NOPE + MLA_ROPE, ts)
    q_scale = ((MLA_NOPE + MLA_ROPE) ** -0.5) * LOG2E
    x1 = qT[:, MLA_NOPE:MLA_NOPE + half]
    x2 = qT[:, MLA_NOPE + half:]
    cos = cosT_ref[...][None]
    sin = sinT_ref[...][None]
    qT_ref[0, :, 0, 0:MLA_NOPE, :] = (qT[:, :MLA_NOPE] * q_scale).astype(BF16)
    qT_ref[0, :, 0, MLA_NOPE:MLA_NOPE + half, :] = ((x1 * cos - x2 * sin) * q_scale).astype(BF16)
    qT_ref[0, :, 0, MLA_NOPE + half:MLA_NOPE + MLA_ROPE, :] = ((x1 * sin + x2 * cos) * q_scale).astype(BF16)
    qT_ref[0, :, 0, MLA_NOPE + MLA_ROPE:, :] = jnp.zeros((heads, MLA_DK - MLA_NOPE - MLA_ROPE, ts), BF16)


def _mla_in(x, kv_in_g, q_in_g, w_dkv, kv_norm, w_uk, w_uv, w_dq, q_norm, w_uq):
    bsz, seq, d = x.shape
    ts = SEQ_TILE
    nt = seq // ts
    heads = MLA_HEADS
    half = MLA_ROPE // 2

    pos = jnp.arange(seq, dtype=F32)
    inv = ROPE_THETA ** (-jnp.arange(0, MLA_ROPE, 2, dtype=F32) / MLA_ROPE)
    ang = pos[:, None] * inv[None, :]
    cos, sin = jnp.cos(ang), jnp.sin(ang)
    zpad = jnp.zeros((seq, LANES - MLA_ROPE), F32)
    kcos = jnp.concatenate([cos, cos, zpad], axis=1)
    ksin = jnp.concatenate([-sin, sin, zpad], axis=1)
    cosT, sinT = cos.T, sin.T

    zcol = jnp.zeros((d, LANES - MLA_ROPE), F32)
    wd = jnp.concatenate([
        w_dkv, zcol,
        w_dkv[:, KV_LORA + half:], w_dkv[:, KV_LORA:KV_LORA + half], zcol], axis=1).astype(BF16)

    kern = functools.partial(_mla_in_kernel, ts=ts)
    return pl.pallas_call(
        kern,
        grid=(bsz, nt),
        in_specs=[
            pl.BlockSpec((1, ts, d), lambda b, s: (b, s, 0)),
            _const_spec((1, d)),
            _const_spec((1, d)),
            _const_spec(wd.shape),
            _const_spec((1, KV_LORA)),
            pl.BlockSpec((ts, LANES), lambda b, s: (s, 0)),
            pl.BlockSpec((ts, LANES), lambda b, s: (s, 0)),
            _const_spec((KV_LORA, heads * MLA_NOPE)),
            _const_spec((heads * MLA_V, KV_LORA)),
            _const_spec((d, Q_LORA)),
            _const_spec((1, Q_LORA)),
            _const_spec((heads * (MLA_NOPE + MLA_ROPE), Q_LORA)),
            pl.BlockSpec((half, ts), lambda b, s: (0, s)),
            pl.BlockSpec((half, ts), lambda b, s: (0, s)),
        ],
        out_specs=[
            pl.BlockSpec((1, heads, 1, MLA_DK, ts), lambda b, s: (b, 0, s, 0, 0)),
            pl.BlockSpec((1, heads, 1, ts, MLA_DK), lambda b, s: (b, 0, s, 0, 0)),
            pl.BlockSpec((1, heads, 1, MLA_DVA, ts), lambda b, s: (b, 0, s, 0, 0)),
        ],
        out_shape=[
            jax.ShapeDtypeStruct((bsz, heads, nt, MLA_DK, ts), BF16),
            jax.ShapeDtypeStruct((bsz, heads, nt, ts, MLA_DK), BF16),
            jax.ShapeDtypeStruct((bsz, heads, nt, MLA_DVA, ts), BF16),
        ],
        compiler_params=_params(("arbitrary", "arbitrary")),
        name="mla_in",
    )(x, kv_in_g.reshape(1, d), q_in_g.reshape(1, d), wd, kv_norm.reshape(1, KV_LORA), kcos, ksin,
      w_uk.astype(BF16), w_uv.T.astype(BF16), w_dq.astype(BF16), q_norm.reshape(1, Q_LORA),
      w_uq.T.astype(BF16), cosT, sinT)


def kernel(x, fox_norm, w_fox_in, b_fox_f, w_fox_out, kv_in_norm, w_dkv, kv_norm, w_uk, w_uv, mla_norm, w_dq, q_norm, w_uq, w_mla_out, ffn_norm, w_ffn_in, ffn_conv_w, ffn_conv_b, w_ffn_out, final_norm):
    depth = ffn_norm.shape[0]
    n_a = fox_norm.shape[0]
    kv = None
    for layer in range(depth):
        last = layer == depth - 1
        if layer < n_a:
            qT, k, vT = _fox_in(x, fox_norm[layer], w_fox_in[layer], b_fox_f[layer])
            oT = _flash(qT, k, vT, dv=FOX_HEAD_DIM, chunk=1, name="fox_attn")
            w_o = w_fox_out[layer]
        else:
            j = layer - n_a
            qT, k_new, vT_new = _mla_in(x, kv_in_norm, mla_norm[j], w_dkv, kv_norm, w_uk, w_uv,
                                        w_dq[j], q_norm[j], w_uq[j])
            if kv is None:
                kv = (k_new, vT_new)
            oT = _flash(qT, kv[0], kv[1], dv=MLA_V, chunk=CHUNK, name="mla_attn")
            w_o = w_mla_out[j]
        x = _attn_out_ffn(x, oT, w_o, ffn_norm[layer], w_ffn_in[layer], ffn_conv_w[layer],
                          ffn_conv_b[layer], w_ffn_out[layer], final_norm, final_norm=last)
    return x
```
